```python
import math
import jax, jax.numpy as jnp
from jax import lax
import numpy as np

D_MODEL = 2048
BATCH = 1
SEQ = 16384
DEPTH = 1
DEC_BATCH = 4
DEC_SEQ = 2048
PAST_LEN = 128

ATTN_WIDTH = D_MODEL // 2
HEAD_DIM = 64
N_HEADS = ATTN_WIDTH // HEAD_DIM
POOL_WIDTH = D_MODEL - ATTN_WIDTH
POOL_WINDOWS = (2, 4, 8, 16)
N_POOL_GROUPS = len(POOL_WINDOWS)
POOL_GROUP = POOL_WIDTH // N_POOL_GROUPS
DILATED_PATTERNS = ((128, 1), (512, 4), (2048, 16))
D_FF = 4 * D_MODEL
IN_WIDTH = 3 * ATTN_WIDTH + POOL_WIDTH
EPS = 1e-6
NEG = -1e30

kernel_name = "hybrid_dilated_attn_multiscale_pool_encoder"


def _rmsnorm(x, g):
    xf = x.astype(jnp.float32)
    y = xf * lax.rsqrt(jnp.mean(xf * xf, axis=-1, keepdims=True) + EPS)
    return (y * g.astype(jnp.float32)).astype(x.dtype)


def _alibi_slopes(n_heads):
    return jnp.asarray([2.0 ** (-8.0 * (h + 1) / n_heads) for h in range(n_heads)], dtype=jnp.float32)


def _block_size(L, r):
    bs = 1
    while bs * 2 <= r and L % (bs * 2) == 0:
        bs *= 2
    return bs


def _dilated_branch(q, k, v, slopes, window, dilation):
    B, T, H, Dh = q.shape
    d = dilation
    r = window // (2 * d)
    L = T // d
    bs = _block_size(L, r)
    nb = -(-r // bs)
    nblk = L // bs
    Kw = (2 * nb + 1) * bs
    pad = nb * bs

    def split(a):
        return a.reshape(B, L, d, H, Dh).transpose(0, 2, 3, 1, 4)

    qb = split(q).astype(jnp.float32).reshape(B, d, H, nblk, bs, Dh)
    kp = jnp.pad(split(k), ((0, 0), (0, 0), (0, 0), (pad, pad), (0, 0)))
    vp = jnp.pad(split(v), ((0, 0), (0, 0), (0, 0), (pad, pad), (0, 0)))
    idx = jnp.arange(nblk)[:, None] * bs + jnp.arange(Kw)[None, :]
    kw = kp[:, :, :, idx, :].astype(jnp.float32)
    vw = vp[:, :, :, idx, :].astype(jnp.float32)

    s = jnp.einsum('bchnqe,bchnke->bchnqk', qb, kw)
    qpos = jnp.arange(nblk)[:, None] * bs + jnp.arange(bs)[None, :]
    kpos = idx - pad
    dist = jnp.abs(qpos[:, :, None] - kpos[:, None, :])
    valid = (dist <= r) & (kpos[:, None, :] >= 0) & (kpos[:, None, :] < L)
    bias = -slopes[:, None, None, None] * (dist * d).astype(jnp.float32)[None]
    s = jnp.where(valid[None, None, None], s + bias[None, None], NEG)
    m = jnp.max(s, axis=-1, keepdims=True)
    p = jnp.exp(s - m)
    den = jnp.sum(p, axis=-1, keepdims=True)
    o = jnp.einsum('bchnqk,bchnke->bchnqe', p, vw) / den
    lse = (m + jnp.log(den))[..., 0]
    o = o.reshape(B, d, H, L, Dh).transpose(0, 3, 1, 2, 4).reshape(B, T, H, Dh)
    lse = lse.reshape(B, d, H, L).transpose(0, 3, 1, 2).reshape(B, T, H)
    return o, lse


def _dilated_attention(q, k, v):
    slopes = _alibi_slopes(q.shape[2])
    outs, lses = [], []
    for window, dil in DILATED_PATTERNS:
        o, lse = _dilated_branch(q, k, v, slopes, window, dil)
        outs.append(o)
        lses.append(lse)
    w = jax.nn.softmax(jnp.stack(lses, axis=0), axis=0)
    out = jnp.sum(w[..., None] * jnp.stack(outs, axis=0), axis=0)
    return out.astype(q.dtype)


def _multiscale_pool(u, pool_w, pool_scale):
    B, T, C = u.shape
    uf = u.astype(jnp.float32)
    cs = jnp.concatenate([jnp.zeros((B, 1, C), jnp.float32), jnp.cumsum(uf, axis=1)], axis=1)
    ug = uf.reshape(B, T, N_POOL_GROUPS, POOL_GROUP)
    csg = cs.reshape(B, T + 1, N_POOL_GROUPS, POOL_GROUP)
    t = jnp.arange(T)
    outs = []
    for g, w in enumerate(POOL_WINDOWS):
        lo = jnp.clip(t - w // 2, 0, T)
        hi = jnp.clip(t + w // 2, 0, T)
        ssum = csg[:, hi, g, :] - csg[:, lo, g, :]
        cnt = (hi - lo).astype(jnp.float32)[None, :, None]
        outs.append(ssum / cnt - ug[:, :, g, :])
    pooled = jnp.stack(outs, axis=2)
    y = jnp.einsum('btgc,gcd->btgd', pooled, pool_w.astype(jnp.float32)).reshape(B, T, C)
    return (y * pool_scale.astype(jnp.float32)).astype(u.dtype)


def _layer(x, ln1_g, w_in, pool_w, pool_scale, w_out, ln2_g, w1, w2):
    B, T, _ = x.shape
    h = _rmsnorm(x, ln1_g)
    proj = h @ w_in
    q = proj[..., :ATTN_WIDTH].reshape(B, T, N_HEADS, HEAD_DIM) * (HEAD_DIM ** -0.5)
    k = proj[..., ATTN_WIDTH:2 * ATTN_WIDTH].reshape(B, T, N_HEADS, HEAD_DIM)
    v = proj[..., 2 * ATTN_WIDTH:3 * ATTN_WIDTH].reshape(B, T, N_HEADS, HEAD_DIM)
    u = proj[..., 3 * ATTN_WIDTH:]
    attn = _dilated_attention(q, k, v).reshape(B, T, ATTN_WIDTH)
    pool = _multiscale_pool(u, pool_w, pool_scale)
    x = x + jnp.concatenate([attn, pool], axis=-1) @ w_out
    h = _rmsnorm(x, ln2_g)
    x = x + jnp.square(jax.nn.relu(h @ w1)) @ w2
    return x


def _trunk(x, ln1_g, w_in, pool_w, pool_scale, w_out, ln2_g, w1, w2, final_g):
    for l in range(DEPTH):
        x = _layer(x, ln1_g[l], w_in[l], pool_w[l], pool_scale[l], w_out[l], ln2_g[l], w1[l], w2[l])
    return _rmsnorm(x, final_g)


def setup_inputs(seed: int = 0) -> dict:
    key = jax.random.key(seed)
    ks = jax.random.split(key, 11)
    f32 = jnp.float32
    x_prompt = jax.random.normal(ks[0], (BATCH, SEQ, D_MODEL), f32)
    x_sample = jax.random.normal(ks[1], (DEC_BATCH, DEC_SEQ, D_MODEL), f32)
    ln1_g = 1.0 + 0.02 * jax.random.normal(ks[2], (DEPTH, D_MODEL), f32)
    w_in = jax.random.normal(ks[3], (DEPTH, D_MODEL, IN_WIDTH), f32) * D_MODEL ** -0.5
    pool_w = jax.random.normal(ks[4], (DEPTH, N_POOL_GROUPS, POOL_GROUP, POOL_GROUP), f32) * POOL_GROUP ** -0.5
    pool_scale = 1.0 + 0.1 * jax.random.normal(ks[5], (DEPTH, POOL_WIDTH), f32)
    w_out = jax.random.normal(ks[6], (DEPTH, D_MODEL, D_MODEL), f32) * D_MODEL ** -0.5
    ln2_g = 1.0 + 0.02 * jax.random.normal(ks[7], (DEPTH, D_MODEL), f32)
    w1 = jax.random.normal(ks[8], (DEPTH, D_MODEL, D_FF), f32) * D_MODEL ** -0.5
    w2 = jax.random.normal(ks[9], (DEPTH, D_FF, D_MODEL), f32) * D_FF ** -0.5
    final_g = 1.0 + 0.02 * jax.random.normal(ks[10], (D_MODEL,), f32)
    return {"x_prompt": x_prompt, "x_sample": x_sample, "ln1_g": ln1_g, "w_in": w_in,
            "pool_w": pool_w, "pool_scale": pool_scale, "w_out": w_out, "ln2_g": ln2_g,
            "w1": w1, "w2": w2, "final_g": final_g}


def reference(x_prompt, x_sample, ln1_g, w_in, pool_w, pool_scale, w_out, ln2_g, w1, w2, final_g):
    y_prompt = _trunk(x_prompt, ln1_g, w_in, pool_w, pool_scale, w_out, ln2_g, w1, w2, final_g)
    y_sample = _trunk(x_sample, ln1_g, w_in, pool_w, pool_scale, w_out, ln2_g, w1, w2, final_g)
    return (y_prompt, y_sample)
```

```python
import functools
import math

import jax
import jax.numpy as jnp
from jax import lax
from jax.experimental import pallas as pl
from jax.experimental.pallas import tpu as pltpu

HEAD_DIM = 64
HEADS_PER_LANE_TILE = 2
LANES = 128
DILATED_PATTERNS = ((128, 1), (512, 4), (2048, 16))
POOL_WINDOWS = (2, 4, 8, 16)
POOL_HALO = 8
EPS = 1e-6
NEG = -1e30
VMEM_LIMIT_BYTES = 56 * 1024 * 1024

_F32 = jnp.float32
_BF16 = jnp.bfloat16


def _rms_scale(x):
    return x * lax.rsqrt(jnp.mean(x * x, axis=-1, keepdims=True) + EPS)


def _inproj_body(x_ref, g_ref, w_ref, q_ref, k_ref, v_ref, u_ref, *, attn_width):
    x = x_ref[...]
    h = (_rms_scale(x) * g_ref[...]).astype(_BF16)
    c = attn_width
    q = jnp.dot(h, w_ref[:, 0:c], preferred_element_type=_F32)
    q_ref[...] = (q * (HEAD_DIM ** -0.5)).astype(_BF16)
    k_ref[...] = jnp.dot(h, w_ref[:, c:2 * c], preferred_element_type=_F32).astype(_BF16)
    v_ref[...] = jnp.dot(h, w_ref[:, 2 * c:3 * c], preferred_element_type=_F32).astype(_BF16)
    u_ref[...] = jnp.dot(h, w_ref[:, 3 * c:], preferred_element_type=_F32)


def _inproj(x2d, g, w_in_bf16, attn_width, tm):
    n, d_model = x2d.shape
    in_width = w_in_bf16.shape[1]
    pool_width = in_width - 3 * attn_width
    row = lambda i: (i, 0)
    const = lambda i: (0, 0)
    return pl.pallas_call(
        functools.partial(_inproj_body, attn_width=attn_width),
        grid=(n // tm,),
        in_specs=[
            pl.BlockSpec((tm, d_model), row),
            pl.BlockSpec((1, d_model), const),
            pl.BlockSpec((d_model, in_width), const, pipeline_mode=pl.Buffered(1)),
        ],
        out_specs=[
            pl.BlockSpec((tm, attn_width), row),
            pl.BlockSpec((tm, attn_width), row),
            pl.BlockSpec((tm, attn_width), row),
            pl.BlockSpec((tm, pool_width), row),
        ],
        out_shape=[
            jax.ShapeDtypeStruct((n, attn_width), _BF16),
            jax.ShapeDtypeStruct((n, attn_width), _BF16),
            jax.ShapeDtypeStruct((n, attn_width), _BF16),
            jax.ShapeDtypeStruct((n, pool_width), _F32),
        ],
        compiler_params=pltpu.CompilerParams(
            dimension_semantics=("arbitrary",), vmem_limit_bytes=VMEM_LIMIT_BYTES),
        name="inproj",
    )(x2d, g, w_in_bf16)


def _band_attention_body(q_ref, kp_ref, kc_ref, kn_ref, vp_ref, vc_ref, vn_ref, o_ref, lse_ref,
                         *, radius, dilation, seq_len, n_heads):
    bq = q_ref.shape[1]
    kw = bq + 2 * radius
    start = pl.program_id(2) * bq
    row = lax.broadcasted_iota(jnp.int32, (bq, kw), 0)
    col = lax.broadcasted_iota(jnp.int32, (bq, kw), 1)
    offset = col - radius - row
    key_pos = start + col - radius
    valid = (jnp.abs(offset) <= radius) & (key_pos >= 0) & (key_pos < seq_len)
    neg_dist = -(jnp.abs(offset) * dilation).astype(_F32)
    lane = lax.broadcasted_iota(jnp.int32, (bq, LANES), 1)
    first_head = lane < HEAD_DIM

    for pair in range(n_heads // HEADS_PER_LANE_TILE):
        lanes = slice(pair * LANES, (pair + 1) * LANES)
        q = q_ref[0, :, lanes]
        k = jnp.concatenate([kp_ref[0, :, lanes], kc_ref[0, :, lanes], kn_ref[0, :, lanes]], axis=0)
        v = jnp.concatenate([vp_ref[0, :, lanes], vc_ref[0, :, lanes], vn_ref[0, :, lanes]], axis=0)
        outs, lses = [], []
        for sub in range(HEADS_PER_LANE_TILE):
            head = pair * HEADS_PER_LANE_TILE + sub
            slope = 2.0 ** (-8.0 * (head + 1) / n_heads)
            head_lanes = first_head if sub == 0 else ~first_head
            qh = jnp.where(head_lanes, q, jnp.zeros_like(q))
            s = lax.dot_general(qh, k, (((1,), (1,)), ((), ())), preferred_element_type=_F32)
            s = jnp.where(valid, s + slope * neg_dist, NEG)
            m = jnp.max(s, axis=-1, keepdims=True)
            p = jnp.exp(s - m)
            den = jnp.sum(p, axis=-1, keepdims=True)
            pv = jnp.dot(p.astype(_BF16), v, preferred_element_type=_F32)
            outs.append(pv / den)
            lses.append(m + jnp.log(den))
        o_ref[0, :, lanes] = jnp.where(first_head, outs[0], outs[1]).astype(o_ref.dtype)
        lse_ref[0, :, lanes] = jnp.where(first_head, lses[0], lses[1])


def _band_attention(q, k, v, dilation, radius, bq):
    b, t, c = q.shape
    seq = t // dilation
    n_heads = c // HEAD_DIM
    view = lambda a: a.reshape(b, seq, dilation * c)
    halo_blocks_per_q = bq // radius
    n_halo_blocks = seq // radius
    cur = lambda bi, ci, i: (bi, i, ci)
    prev = lambda bi, ci, i: (bi, jnp.maximum(i * halo_blocks_per_q - 1, 0), ci)
    nxt = lambda bi, ci, i: (bi, jnp.minimum((i + 1) * halo_blocks_per_q, n_halo_blocks - 1), ci)
    blk = lambda rows, imap: pl.BlockSpec((1, rows, c), imap)
    o, lse = pl.pallas_call(
        functools.partial(_band_attention_body, radius=radius, dilation=dilation, seq_len=seq,
                          n_heads=n_heads),
        grid=(b, dilation, seq // bq),
        in_specs=[blk(bq, cur), blk(radius, prev), blk(bq, cur), blk(radius, nxt),
                  blk(radius, prev), blk(bq, cur), blk(radius, nxt)],
        out_specs=[blk(bq, cur), blk(bq, cur)],
        out_shape=[jax.ShapeDtypeStruct((b, seq, dilation * c), _BF16),
                   jax.ShapeDtypeStruct((b, seq, dilation * c), _F32)],
        compiler_params=pltpu.CompilerParams(
            dimension_semantics=("arbitrary", "arbitrary", "arbitrary"),
            vmem_limit_bytes=VMEM_LIMIT_BYTES),
        name=f"band_attention_d{dilation}",
    )(view(q), view(k), view(k), view(k), view(v), view(v), view(v))
    return o.reshape(b, t, c), lse.reshape(b, t, c)


def _mix_outproj_body(o1_ref, o2_ref, o3_ref, l1_ref, l2_ref, l3_ref, u_ref, up_ref, un_ref, x_ref,
                      pw_ref, ps_ref, wo_ref, out_ref, ext_ref, *, seq_len):
    tm, attn_width = o1_ref.shape
    pool_width = u_ref.shape[1]
    group = pool_width // len(POOL_WINDOWS)

    l1, l2, l3 = l1_ref[...], l2_ref[...], l3_ref[...]
    top = jnp.maximum(jnp.maximum(l1, l2), l3)
    e1, e2, e3 = jnp.exp(l1 - top), jnp.exp(l2 - top), jnp.exp(l3 - top)
    mixed = (e1 * o1_ref[...].astype(_F32) + e2 * o2_ref[...].astype(_F32)
             + e3 * o3_ref[...].astype(_F32)) / (e1 + e2 + e3)
    acc = x_ref[...] + jnp.dot(mixed.astype(_BF16), wo_ref[0:attn_width, :],
                               preferred_element_type=_F32)

    pos0 = (pl.program_id(0) * tm) % seq_len
    at_start = pos0 == 0
    at_end = pos0 + tm == seq_len
    ext_ref[0:POOL_HALO, :] = jnp.where(at_start, 0.0, up_ref[...])
    ext_ref[POOL_HALO:POOL_HALO + tm, :] = u_ref[...]
    ext_ref[POOL_HALO + tm:, :] = jnp.where(at_end, 0.0, un_ref[...])
    pos = pos0 + lax.broadcasted_iota(jnp.int32, (tm, 1), 0)
    for g, w in enumerate(POOL_WINDOWS):
        cols = slice(g * group, (g + 1) * group)
        half = w // 2
        total = ext_ref[POOL_HALO - half:POOL_HALO - half + tm, cols]
        for j in range(-half + 1, half):
            total = total + ext_ref[POOL_HALO + j:POOL_HALO + j + tm, cols]
        count = (jnp.minimum(pos + half, seq_len) - jnp.maximum(pos - half, 0)).astype(_F32)
        pooled = total / count - u_ref[:, cols]
        y = jnp.dot(pooled.astype(_BF16), pw_ref[g], preferred_element_type=_F32) * ps_ref[:, cols]
        acc = acc + jnp.dot(y.astype(_BF16), wo_ref[attn_width + g * group:attn_width + (g + 1) * group, :],
                            preferred_element_type=_F32)
    out_ref[...] = acc


def _mix_outproj(o, lse, u, x2d, pool_w_bf16, pool_scale, w_out_bf16, seq_len, tm):
    n, d_model = x2d.shape
    attn_width = o[0].shape[1]
    pool_width = u.shape[1]
    halo_per_tile = tm // POOL_HALO
    n_halo = n // POOL_HALO
    row = lambda i: (i, 0)
    prev = lambda i: (jnp.maximum(i * halo_per_tile - 1, 0), 0)
    nxt = lambda i: (jnp.minimum((i + 1) * halo_per_tile, n_halo - 1), 0)
    const2 = lambda i: (0, 0)
    const3 = lambda i: (0, 0, 0)
    attn_spec = pl.BlockSpec((tm, attn_width), row)
    return pl.pallas_call(
        functools.partial(_mix_outproj_body, seq_len=seq_len),
        grid=(n // tm,),
        in_specs=[attn_spec] * 6 + [
            pl.BlockSpec((tm, pool_width), row),
            pl.BlockSpec((POOL_HALO, pool_width), prev),
            pl.BlockSpec((POOL_HALO, pool_width), nxt),
            pl.BlockSpec((tm, d_model), row),
            pl.BlockSpec(pool_w_bf16.shape, const3, pipeline_mode=pl.Buffered(1)),
            pl.BlockSpec((1, pool_width), const2),
            pl.BlockSpec(w_out_bf16.shape, const2, pipeline_mode=pl.Buffered(1)),
        ],
        out_specs=pl.BlockSpec((tm, d_model), row),
        out_shape=jax.ShapeDtypeStruct((n, d_model), _F32),
        scratch_shapes=[pltpu.VMEM((tm + 2 * POOL_HALO, pool_width), _F32)],
        compiler_params=pltpu.CompilerParams(
            dimension_semantics=("arbitrary",), vmem_limit_bytes=VMEM_LIMIT_BYTES),
        name="mix_outproj",
    )(*o, *lse, u, u, u, x2d, pool_w_bf16, pool_scale, w_out_bf16)


def _ffn_body(x_ref, g2_ref, w1_ref, w2_ref, gf_ref, out_ref, h_ref):
    j = pl.program_id(1)

    @pl.when(j == 0)
    def _():
        x = x_ref[...]
        h_ref[...] = (_rms_scale(x) * g2_ref[...]).astype(_BF16)
        out_ref[...] = x

    a = jnp.dot(h_ref[...], w1_ref[...], preferred_element_type=_F32)
    a = jnp.square(jnp.maximum(a, 0.0)).astype(_BF16)
    out_ref[...] += jnp.dot(a, w2_ref[...], preferred_element_type=_F32)

    @pl.when(j == pl.num_programs(1) - 1)
    def _():
        out_ref[...] = _rms_scale(out_ref[...]) * gf_ref[...]


def _ffn(x2d, g2, w1_bf16, w2_bf16, gf, tm, fc):
    n, d_model = x2d.shape
    d_ff = w1_bf16.shape[1]
    return pl.pallas_call(
        _ffn_body,
        grid=(n // tm, d_ff // fc),
        in_specs=[
            pl.BlockSpec((tm, d_model), lambda i, j: (i, 0)),
            pl.BlockSpec((1, d_model), lambda i, j: (0, 0)),
            pl.BlockSpec((d_model, fc), lambda i, j: (0, j)),
            pl.BlockSpec((fc, d_model), lambda i, j: (j, 0)),
            pl.BlockSpec((1, d_model), lambda i, j: (0, 0)),
        ],
        out_specs=pl.BlockSpec((tm, d_model), lambda i, j: (i, 0)),
        out_shape=jax.ShapeDtypeStruct((n, d_model), _F32),
        scratch_shapes=[pltpu.VMEM((tm, d_model), _BF16)],
        compiler_params=pltpu.CompilerParams(
            dimension_semantics=("arbitrary", "arbitrary"), vmem_limit_bytes=VMEM_LIMIT_BYTES),
        name="ffn",
    )(x2d, g2, w1_bf16, w2_bf16, gf)


def _tile_rows(seq_len, target):
    tm = min(seq_len, target)
    assert seq_len % tm == 0
    return tm


def _trunk(x, weights, final_g):
    b, t, d_model = x.shape
    x2d = x.reshape(b * t, d_model)
    for (ln1_g, w_in, pool_w, pool_scale, w_out, ln2_g, w1, w2) in weights:
        attn_width = (w_in.shape[1] - pool_w.shape[0] * pool_w.shape[1]) // 3
        tm = _tile_rows(t, 512)
        q, k, v, u = _inproj(x2d, ln1_g, w_in, attn_width, tm)
        q, k, v = (a.reshape(b, t, attn_width) for a in (q, k, v))
        outs, lses = [], []
        for window, dilation in DILATED_PATTERNS:
            radius = window // (2 * dilation)
            o, lse = _band_attention(q, k, v, dilation, radius, bq=128)
            outs.append(o.reshape(b * t, attn_width))
            lses.append(lse.reshape(b * t, attn_width))
        x2d = _mix_outproj(outs, lses, u, x2d, pool_w, pool_scale, w_out, t, _tile_rows(t, 256))
        x2d = _ffn(x2d, ln2_g, w1, w2, final_g, tm, fc=512)
    return x2d.reshape(b, t, d_model)


def kernel(x_prompt, x_sample, ln1_g, w_in, pool_w, pool_scale, w_out, ln2_g, w1, w2, final_g):
    depth = w_in.shape[0]
    assert depth == 1, "the fused final norm assumes a single layer"
    weights = [(
        ln1_g[l][None, :], w_in[l].astype(_BF16), pool_w[l].astype(_BF16), pool_scale[l][None, :],
        w_out[l].astype(_BF16), ln2_g[l][None, :], w1[l].astype(_BF16), w2[l].astype(_BF16),
    ) for l in range(depth)]
    gf = final_g[None, :]
    return (_trunk(x_prompt, weights, gf), _trunk(x_sample, weights, gf))
```

```python
import functools

import jax
import jax.numpy as jnp
from jax import lax
from jax.experimental import pallas as pl
from jax.experimental.pallas import tpu as pltpu

HEAD_DIM = 64
LANES = 128
HEADS_PER_LANE_TILE = LANES // HEAD_DIM
DILATED_PATTERNS = ((128, 1), (512, 4), (2048, 16))
ATTN_QUERY_ROWS = 128
GATHER_STRIDE = 4
POOL_WINDOWS = (2, 4, 8, 16)
POOL_HALO = 8
EPS = 1e-6
NEG = -1e30
VMEM_LIMIT_BYTES = 56 * 1024 * 1024

_F32 = jnp.float32
_BF16 = jnp.bfloat16


def _rms_scale(x):
    return x * lax.rsqrt(jnp.mean(x * x, axis=-1, keepdims=True) + EPS)


def _inproj_body(x_ref, g_ref, w_ref, q_ref, k_ref, v_ref, u_ref, *, attn_width):
    x = x_ref[...]
    h = (_rms_scale(x) * g_ref[...]).astype(_BF16)
    c = attn_width
    q = jnp.dot(h, w_ref[:, 0:c], preferred_element_type=_F32) * (HEAD_DIM ** -0.5)
    k = jnp.dot(h, w_ref[:, c:2 * c], preferred_element_type=_F32)
    v = jnp.dot(h, w_ref[:, 2 * c:3 * c], preferred_element_type=_F32)
    for p in range(c // LANES):
        lanes = slice(p * LANES, (p + 1) * LANES)
        q_ref[0, p] = q[:, lanes]
        k_ref[0, p] = k[:, lanes]
        v_ref[0, p] = v[:, lanes]
    u_ref[...] = jnp.dot(h, w_ref[:, 3 * c:], preferred_element_type=_F32)


def _inproj(x2d, g, w_in_bf16, attn_width, batch, seq_len, tm):
    n, d_model = x2d.shape
    in_width = w_in_bf16.shape[1]
    pool_width = in_width - 3 * attn_width
    n_pairs = attn_width // LANES
    tiles_per_seq = seq_len // tm
    row = lambda i: (i, 0)
    const = lambda i: (0, 0)
    qkv_spec = pl.BlockSpec((1, n_pairs, tm, LANES), lambda i: (i // tiles_per_seq, 0, i % tiles_per_seq, 0))
    qkv_shape = jax.ShapeDtypeStruct((batch, n_pairs, seq_len, LANES), _F32)
    return pl.pallas_call(
        functools.partial(_inproj_body, attn_width=attn_width),
        grid=(n // tm,),
        in_specs=[
            pl.BlockSpec((tm, d_model), row),
            pl.BlockSpec((1, d_model), const),
            pl.BlockSpec((d_model, in_width), const, pipeline_mode=pl.Buffered(1)),
        ],
        out_specs=[qkv_spec, qkv_spec, qkv_spec, pl.BlockSpec((tm, pool_width), row)],
        out_shape=[qkv_shape, qkv_shape, qkv_shape, jax.ShapeDtypeStruct((n, pool_width), _F32)],
        compiler_params=pltpu.CompilerParams(
            dimension_semantics=("arbitrary",), vmem_limit_bytes=VMEM_LIMIT_BYTES),
        name="inproj",
    )(x2d, g, w_in_bf16)


def _dilated_attention_body(q_ref, kp_ref, kc_ref, kn_ref, vp_ref, vc_ref, vn_ref, out_ref,
                            kd_refs, vd_refs, o_refs, lse_refs, kq_ref, vq_ref, bias_ref,
                            *, seq_len, n_heads, patterns, unroll):
    tb = q_ref.shape[2]
    halo_rows = kp_ref.shape[2]
    bq = ATTN_QUERY_ROWS
    n_stack = HEADS_PER_LANE_TILE
    pair = pl.program_id(1)
    block = pl.program_id(2)
    radius = patterns[0][0] // (2 * patterns[0][1])
    assert all(w // (2 * d) == radius for w, d in patterns)
    kw = bq + 2 * radius
    dilations = [d for _, d in patterns]
    assert dilations == [1, GATHER_STRIDE, GATHER_STRIDE ** 2]

    q_rows = halo_rows // GATHER_STRIDE
    c_rows = tb // GATHER_STRIDE
    for src_p, src_c, src_n, quarter, (d1_ref, d4_ref, d16_ref) in (
            (kp_ref, kc_ref, kn_ref, kq_ref, kd_refs), (vp_ref, vc_ref, vn_ref, vq_ref, vd_refs)):
        d1_ref[0, 0:radius] = src_p[0, 0, halo_rows - radius:halo_rows, :].astype(_BF16)
        d1_ref[0, radius:radius + tb] = src_c[0, 0].astype(_BF16)
        d1_ref[0, radius + tb:] = src_n[0, 0, 0:radius, :].astype(_BF16)
        for c in range(GATHER_STRIDE):
            quarter[c, 0:q_rows] = src_p[0, 0, pl.ds(c, q_rows, stride=GATHER_STRIDE), :]
            quarter[c, q_rows:q_rows + c_rows] = src_c[0, 0, pl.ds(c, c_rows, stride=GATHER_STRIDE), :]
            quarter[c, q_rows + c_rows:] = src_n[0, 0, pl.ds(c, q_rows, stride=GATHER_STRIDE), :]
            d4_ref[c] = quarter[c, q_rows - radius:q_rows + c_rows + radius, :].astype(_BF16)
        for c in range(GATHER_STRIDE ** 2):
            d16_ref[c] = quarter[c % GATHER_STRIDE,
                                 pl.ds(c // GATHER_STRIDE, d16_ref.shape[1], stride=GATHER_STRIDE), :].astype(_BF16)

    row = lax.broadcasted_iota(jnp.int32, (bq, kw), 0)
    col = lax.broadcasted_iota(jnp.int32, (bq, kw), 1)
    dist = jnp.abs(col - radius - row)
    first_head_index = (pair * n_stack + 1).astype(_F32)
    slopes = [jnp.exp2((jnp.full((1, LANES), sub, _F32) + first_head_index) * (-8.0 / n_heads))[:, 0:1]
              for sub in range(n_stack)]
    for case in range(4):
        ok = dist <= radius
        if case & 1:
            ok = ok & (col >= radius)
        if case & 2:
            ok = ok & (col < radius + bq)
        masked_dist = jnp.where(ok, -dist.astype(_F32), -jnp.inf)
        for br, d in enumerate(dilations):
            for sub in range(n_stack):
                bias_ref[br * 4 + case, sub * bq:(sub + 1) * bq, :] = masked_dist * (slopes[sub] * float(d))

    lane = lax.broadcasted_iota(jnp.int32, (bq, LANES), 1)
    first_head = lane < HEAD_DIM

    for br, d in enumerate(dilations):
        n_sub = tb // (d * bq)
        subs_per_class = seq_len // (d * bq)
        kd_ref, vd_ref, o_ref, lse_ref = kd_refs[br], vd_refs[br], o_refs[br], lse_refs[br]

        def sub_block(n, carry, br=br, d=d, n_sub=n_sub, subs_per_class=subs_per_class, kd_ref=kd_ref,
                      vd_ref=vd_ref, o_ref=o_ref, lse_ref=lse_ref):
            c = n // n_sub
            j = n % n_sub
            q_start = c + d * bq * j
            key_row0 = pl.multiple_of(j * bq, bq)
            sub_in_seq = block * n_sub + j
            case = ((sub_in_seq == 0).astype(jnp.int32)
                    + 2 * (sub_in_seq == subs_per_class - 1).astype(jnp.int32))
            q = q_ref[0, 0, pl.ds(q_start, bq, stride=d), :].astype(_BF16)
            zero = jnp.zeros_like(q)
            q2 = jnp.concatenate([jnp.where(first_head, q, zero), jnp.where(first_head, zero, q)], axis=0)
            k = kd_ref[c, pl.ds(key_row0, kw), :]
            v = vd_ref[c, pl.ds(key_row0, kw), :]
            s = lax.dot_general(q2, k, (((1,), (1,)), ((), ())), preferred_element_type=_F32)
            s = s + bias_ref[br * 4 + case]
            m = jnp.max(s, axis=-1, keepdims=True)
            p = jnp.exp(s - m)
            den = jnp.sum(p, axis=-1, keepdims=True)
            pv = jnp.dot(p.astype(_BF16), v, preferred_element_type=_F32) / den
            lse = m + jnp.log(den)
            o_ref[pl.ds(q_start, bq, stride=d), :] = jnp.where(first_head, pv[0:bq], pv[bq:])
            lse_ref[pl.ds(q_start, bq, stride=d), :] = jnp.where(first_head, lse[0:bq], lse[bq:])
            return carry

        lax.fori_loop(0, d * n_sub, sub_block, 0, unroll=unroll)

    lses = [r[...] for r in lse_refs]
    top = functools.reduce(jnp.maximum, lses)
    weights = [jnp.exp(l - top) for l in lses]
    total = functools.reduce(jnp.add, [w * r[...] for w, r in zip(weights, o_refs)])
    out_ref[0] = (total / functools.reduce(jnp.add, weights)).astype(out_ref.dtype)


def _dilated_attention(q, k, v, seq_len, tb, unroll):
    b, n_pairs, t, _ = q.shape
    n_heads = n_pairs * HEADS_PER_LANE_TILE
    bq = ATTN_QUERY_ROWS
    radius = DILATED_PATTERNS[0][0] // (2 * DILATED_PATTERNS[0][1])
    halo_rows = max((w // 2) for w, _ in DILATED_PATTERNS)
    assert tb % (bq * max(d for _, d in DILATED_PATTERNS)) == 0 and t % tb == 0 and tb % halo_rows == 0
    halo_per_block = tb // halo_rows
    n_halo = t // halo_rows
    cur = lambda bi, p, i: (bi, p, i, 0)
    prev = lambda bi, p, i: (bi, p, jnp.maximum(i * halo_per_block - 1, 0), 0)
    nxt = lambda bi, p, i: (bi, p, jnp.minimum((i + 1) * halo_per_block, n_halo - 1), 0)
    blk = lambda rows, imap: pl.BlockSpec((1, 1, rows, LANES), imap)
    gathered = [pltpu.VMEM((d, tb // d + 2 * radius, LANES), _BF16) for _, d in DILATED_PATTERNS]
    per_branch = [pltpu.VMEM((tb, LANES), _F32) for _ in DILATED_PATTERNS]
    quarter = pltpu.VMEM((GATHER_STRIDE, (tb + 2 * halo_rows) // GATHER_STRIDE, LANES), _F32)
    bias = pltpu.VMEM((4 * len(DILATED_PATTERNS), HEADS_PER_LANE_TILE * bq, bq + 2 * radius), _F32)
    return pl.pallas_call(
        functools.partial(_dilated_attention_body, seq_len=seq_len, n_heads=n_heads,
                          patterns=DILATED_PATTERNS, unroll=unroll),
        grid=(b, n_pairs, t // tb),
        in_specs=[blk(tb, cur), blk(halo_rows, prev), blk(tb, cur), blk(halo_rows, nxt),
                  blk(halo_rows, prev), blk(tb, cur), blk(halo_rows, nxt)],
        out_specs=pl.BlockSpec((1, tb, LANES), lambda bi, p, i: (bi, i, p)),
        out_shape=jax.ShapeDtypeStruct((b, t, n_pairs * LANES), _BF16),
        scratch_shapes=[gathered, gathered, per_branch, per_branch, quarter, quarter, bias],
        compiler_params=pltpu.CompilerParams(
            dimension_semantics=("arbitrary", "arbitrary", "arbitrary"),
            vmem_limit_bytes=VMEM_LIMIT_BYTES),
        name="dilated_attention",
    )(q, k, k, k, v, v, v)


def _pool_outproj_body(attn_ref, u_ref, up_ref, un_ref, x_ref, pw_ref, ps_ref, wo_ref, out_ref, ext_ref,
                       *, seq_len):
    tm, attn_width = attn_ref.shape
    pool_width = u_ref.shape[1]
    group = pool_width // len(POOL_WINDOWS)
    acc = x_ref[...] + jnp.dot(attn_ref[...], wo_ref[0:attn_width, :], preferred_element_type=_F32)

    pos0 = (pl.program_id(0) * tm) % seq_len
    at_start = pos0 == 0
    at_end = pos0 + tm == seq_len
    ext_ref[0:POOL_HALO, :] = jnp.where(at_start, 0.0, up_ref[...])
    ext_ref[POOL_HALO:POOL_HALO + tm, :] = u_ref[...]
    ext_ref[POOL_HALO + tm:, :] = jnp.where(at_end, 0.0, un_ref[...])
    pos = pos0 + lax.broadcasted_iota(jnp.int32, (tm, 1), 0)
    for g, w in enumerate(POOL_WINDOWS):
        cols = slice(g * group, (g + 1) * group)
        half = w // 2
        total = ext_ref[POOL_HALO - half:POOL_HALO - half + tm, cols]
        for j in range(-half + 1, half):
            total = total + ext_ref[POOL_HALO + j:POOL_HALO + j + tm, cols]
        count = (jnp.minimum(pos + half, seq_len) - jnp.maximum(pos - half, 0)).astype(_F32)
        pooled = total / count - u_ref[:, cols]
        y = jnp.dot(pooled.astype(_BF16), pw_ref[g], preferred_element_type=_F32) * ps_ref[:, cols]
        acc = acc + jnp.dot(y.astype(_BF16), wo_ref[attn_width + g * group:attn_width + (g + 1) * group, :],
                            preferred_element_type=_F32)
    out_ref[...] = acc


def _pool_outproj(attn, u, x2d, pool_w_bf16, pool_scale, w_out_bf16, seq_len, tm):
    n, d_model = x2d.shape
    attn_width = attn.shape[1]
    pool_width = u.shape[1]
    halo_per_tile = tm // POOL_HALO
    n_halo = n // POOL_HALO
    row = lambda i: (i, 0)
    prev = lambda i: (jnp.maximum(i * halo_per_tile - 1, 0), 0)
    nxt = lambda i: (jnp.minimum((i + 1) * halo_per_tile, n_halo - 1), 0)
    const2 = lambda i: (0, 0)
    const3 = lambda i: (0, 0, 0)
    return pl.pallas_call(
        functools.partial(_pool_outproj_body, seq_len=seq_len),
        grid=(n // tm,),
        in_specs=[
            pl.BlockSpec((tm, attn_width), row),
            pl.BlockSpec((tm, pool_width), row),
            pl.BlockSpec((POOL_HALO, pool_width), prev),
            pl.BlockSpec((POOL_HALO, pool_width), nxt),
            pl.BlockSpec((tm, d_model), row),
            pl.BlockSpec(pool_w_bf16.shape, const3, pipeline_mode=pl.Buffered(1)),
            pl.BlockSpec((1, pool_width), const2),
            pl.BlockSpec(w_out_bf16.shape, const2, pipeline_mode=pl.Buffered(1)),
        ],
        out_specs=pl.BlockSpec((tm, d_model), row),
        out_shape=jax.ShapeDtypeStruct((n, d_model), _F32),
        scratch_shapes=[pltpu.VMEM((tm + 2 * POOL_HALO, pool_width), _F32)],
        compiler_params=pltpu.CompilerParams(
            dimension_semantics=("arbitrary",), vmem_limit_bytes=VMEM_LIMIT_BYTES),
        name="pool_outproj",
    )(attn, u, u, u, x2d, pool_w_bf16, pool_scale, w_out_bf16)


def _ffn_body(x_ref, g2_ref, w1_ref, w2_ref, gf_ref, out_ref, h_ref):
    j = pl.program_id(1)

    @pl.when(j == 0)
    def _():
        x = x_ref[...]
        h_ref[...] = (_rms_scale(x) * g2_ref[...]).astype(_BF16)
        out_ref[...] = x

    a = jnp.dot(h_ref[...], w1_ref[...], preferred_element_type=_F32)
    a = jnp.square(jnp.maximum(a, 0.0)).astype(_BF16)
    out_ref[...] += jnp.dot(a, w2_ref[...], preferred_element_type=_F32)

    @pl.when(j == pl.num_programs(1) - 1)
    def _():
        out_ref[...] = _rms_scale(out_ref[...]) * gf_ref[...]


def _ffn(x2d, g2, w1_bf16, w2_bf16, gf, tm, fc):
    n, d_model = x2d.shape
    d_ff = w1_bf16.shape[1]
    return pl.pallas_call(
        _ffn_body,
        grid=(n // tm, d_ff // fc),
        in_specs=[
            pl.BlockSpec((tm, d_model), lambda i, j: (i, 0)),
            pl.BlockSpec((1, d_model), lambda i, j: (0, 0)),
            pl.BlockSpec((d_model, fc), lambda i, j: (0, j)),
            pl.BlockSpec((fc, d_model), lambda i, j: (j, 0)),
            pl.BlockSpec((1, d_model), lambda i, j: (0, 0)),
        ],
        out_specs=pl.BlockSpec((tm, d_model), lambda i, j: (i, 0)),
        out_shape=jax.ShapeDtypeStruct((n, d_model), _F32),
        scratch_shapes=[pltpu.VMEM((tm, d_model), _BF16)],
        compiler_params=pltpu.CompilerParams(
            dimension_semantics=("arbitrary", "arbitrary"), vmem_limit_bytes=VMEM_LIMIT_BYTES),
        name="ffn",
    )(x2d, g2, w1_bf16, w2_bf16, gf)


def _tile_rows(seq_len, target):
    tm = min(seq_len, target)
    assert seq_len % tm == 0
    return tm


def _trunk(x, weights, final_g):
    b, t, d_model = x.shape
    x2d = x.reshape(b * t, d_model)
    for (ln1_g, w_in, pool_w, pool_scale, w_out, ln2_g, w1, w2) in weights:
        attn_width = (w_in.shape[1] - pool_w.shape[0] * pool_w.shape[1]) // 3
        tm = _tile_rows(t, 512)
        q, k, v, u = _inproj(x2d, ln1_g, w_in, attn_width, b, t, tm)
        attn = _dilated_attention(q, k, v, t, _tile_rows(t, 2048), unroll=16).reshape(b * t, attn_width)
        x2d = _pool_outproj(attn, u, x2d, pool_w, pool_scale, w_out, t, tm)
        x2d = _ffn(x2d, ln2_g, w1, w2, final_g, tm, fc=512)
    return x2d.reshape(b, t, d_model)


def kernel(x_prompt, x_sample, ln1_g, w_in, pool_w, pool_scale, w_out, ln2_g, w1, w2, final_g):
    depth = w_in.shape[0]
    assert depth == 1, "the fused final norm assumes a single layer"
    weights = [(
        ln1_g[l][None, :], w_in[l].astype(_BF16), pool_w[l].astype(_BF16), pool_scale[l][None, :],
        w_out[l].astype(_BF16), ln2_g[l][None, :], w1[l].astype(_BF16), w2[l].astype(_BF16),
    ) for l in range(depth)]
    gf = final_g[None, :]
    return (_trunk(x_prompt, weights, gf), _trunk(x_sample, weights, gf))
```

```python
import functools

import jax
import jax.numpy as jnp
from jax import lax
from jax.experimental import pallas as pl
from jax.experimental.pallas import tpu as pltpu

HEAD_DIM = 64
LANES = 128
HEADS_PER_LANE_TILE = LANES // HEAD_DIM
DILATED_PATTERNS = ((128, 1), (512, 4), (2048, 16))
ATTN_QUERY_ROWS = 128
GATHER_STRIDE = 4
POOL_WINDOWS = (2, 4, 8, 16)
POOL_HALO = 8
EPS = 1e-6
LOG2_E = 1.4426950408889634
VMEM_LIMIT_BYTES = 56 * 1024 * 1024

_F32 = jnp.float32
_BF16 = jnp.bfloat16


def _rms_scale(x):
    return x * lax.rsqrt(jnp.mean(x * x, axis=-1, keepdims=True) + EPS)


def _inproj_body(x_ref, g_ref, w_ref, q_ref, k_ref, v_ref, u_ref, *, attn_width):
    x = x_ref[...]
    h = (_rms_scale(x) * g_ref[...]).astype(_BF16)
    c = attn_width
    q = jnp.dot(h, w_ref[:, 0:c], preferred_element_type=_F32) * (HEAD_DIM ** -0.5 * LOG2_E)
    k = jnp.dot(h, w_ref[:, c:2 * c], preferred_element_type=_F32)
    v = jnp.dot(h, w_ref[:, 2 * c:3 * c], preferred_element_type=_F32)
    for p in range(c // LANES):
        lanes = slice(p * LANES, (p + 1) * LANES)
        q_ref[0, p] = q[:, lanes]
        k_ref[0, p] = k[:, lanes]
        v_ref[0, p] = v[:, lanes]
    u_ref[...] = jnp.dot(h, w_ref[:, 3 * c:], preferred_element_type=_F32)


def _inproj(x2d, g, w_in_bf16, attn_width, batch, seq_len, tm):
    n, d_model = x2d.shape
    in_width = w_in_bf16.shape[1]
    pool_width = in_width - 3 * attn_width
    n_pairs = attn_width // LANES
    tiles_per_seq = seq_len // tm
    row = lambda i: (i, 0)
    const = lambda i: (0, 0)
    qkv_spec = pl.BlockSpec((1, n_pairs, tm, LANES), lambda i: (i // tiles_per_seq, 0, i % tiles_per_seq, 0))
    qkv_shape = jax.ShapeDtypeStruct((batch, n_pairs, seq_len, LANES), _F32)
    return pl.pallas_call(
        functools.partial(_inproj_body, attn_width=attn_width),
        grid=(n // tm,),
        in_specs=[
            pl.BlockSpec((tm, d_model), row),
            pl.BlockSpec((1, d_model), const),
            pl.BlockSpec((d_model, in_width), const, pipeline_mode=pl.Buffered(1)),
        ],
        out_specs=[qkv_spec, qkv_spec, qkv_spec, pl.BlockSpec((tm, pool_width), row)],
        out_shape=[qkv_shape, qkv_shape, qkv_shape, jax.ShapeDtypeStruct((n, pool_width), _F32)],
        compiler_params=pltpu.CompilerParams(
            dimension_semantics=("arbitrary",), vmem_limit_bytes=VMEM_LIMIT_BYTES),
        name="inproj",
    )(x2d, g, w_in_bf16)


def _dilated_attention_body(q_ref, kp_ref, kc_ref, kn_ref, vp_ref, vc_ref, vn_ref, out_ref,
                            kd_refs, vd_refs, o_refs, m_refs, den_refs, kq_ref, vq_ref, bias_ref,
                            *, seq_len, n_heads, patterns, unroll):
    tb = q_ref.shape[2]
    halo_rows = kp_ref.shape[2]
    bq = ATTN_QUERY_ROWS
    n_stack = HEADS_PER_LANE_TILE
    pair = pl.program_id(1)
    block = pl.program_id(2)
    radius = patterns[0][0] // (2 * patterns[0][1])
    assert all(w // (2 * d) == radius for w, d in patterns)
    kw = bq + 2 * radius
    dilations = [d for _, d in patterns]
    assert dilations == [1, GATHER_STRIDE, GATHER_STRIDE ** 2]

    q_rows = halo_rows // GATHER_STRIDE
    c_rows = tb // GATHER_STRIDE
    for src_p, src_c, src_n, quarter, (d1_ref, d4_ref, d16_ref) in (
            (kp_ref, kc_ref, kn_ref, kq_ref, kd_refs), (vp_ref, vc_ref, vn_ref, vq_ref, vd_refs)):
        d1_ref[0, 0:radius] = src_p[0, 0, halo_rows - radius:halo_rows, :].astype(_BF16)
        d1_ref[0, radius:radius + tb] = src_c[0, 0].astype(_BF16)
        d1_ref[0, radius + tb:] = src_n[0, 0, 0:radius, :].astype(_BF16)
        for c in range(GATHER_STRIDE):
            quarter[c, 0:q_rows] = src_p[0, 0, pl.ds(c, q_rows, stride=GATHER_STRIDE), :]
            quarter[c, q_rows:q_rows + c_rows] = src_c[0, 0, pl.ds(c, c_rows, stride=GATHER_STRIDE), :]
            quarter[c, q_rows + c_rows:] = src_n[0, 0, pl.ds(c, q_rows, stride=GATHER_STRIDE), :]
            d4_ref[c] = quarter[c, q_rows - radius:q_rows + c_rows + radius, :].astype(_BF16)
        for c in range(GATHER_STRIDE ** 2):
            d16_ref[c] = quarter[c % GATHER_STRIDE,
                                 pl.ds(c // GATHER_STRIDE, d16_ref.shape[1], stride=GATHER_STRIDE), :].astype(_BF16)

    row = lax.broadcasted_iota(jnp.int32, (bq, kw), 0)
    col = lax.broadcasted_iota(jnp.int32, (bq, kw), 1)
    dist = jnp.abs(col - radius - row)
    first_head_index = (pair * n_stack + 1).astype(_F32)
    slopes = [jnp.exp2((jnp.full((1, LANES), sub, _F32) + first_head_index) * (-8.0 / n_heads))[:, 0:1]
              for sub in range(n_stack)]
    for case in range(4):
        ok = dist <= radius
        if case & 1:
            ok = ok & (col >= radius)
        if case & 2:
            ok = ok & (col < radius + bq)
        masked_dist = jnp.where(ok, -dist.astype(_F32), -jnp.inf)
        for br, d in enumerate(dilations):
            for sub in range(n_stack):
                bias_ref[br * 4 + case, sub * bq:(sub + 1) * bq, :] = masked_dist * (slopes[sub] * (d * LOG2_E))

    lane = lax.broadcasted_iota(jnp.int32, (bq, LANES), 1)
    first_head = lane < HEAD_DIM

    for br, d in enumerate(dilations):
        n_sub = tb // (d * bq)
        subs_per_class = seq_len // (d * bq)
        kd_ref, vd_ref = kd_refs[br], vd_refs[br]
        o_ref, m_ref, den_ref = o_refs[br], m_refs[br], den_refs[br]

        def sub_block(n, carry, br=br, d=d, n_sub=n_sub, subs_per_class=subs_per_class, kd_ref=kd_ref,
                      vd_ref=vd_ref, o_ref=o_ref, m_ref=m_ref, den_ref=den_ref):
            c = n // n_sub
            j = n % n_sub
            q_start = c + d * bq * j
            key_row0 = pl.multiple_of(j * bq, bq)
            sub_in_seq = block * n_sub + j
            case = ((sub_in_seq == 0).astype(jnp.int32)
                    + 2 * (sub_in_seq == subs_per_class - 1).astype(jnp.int32))
            q = q_ref[0, 0, pl.ds(q_start, bq, stride=d), :].astype(_BF16)
            zero = jnp.zeros_like(q)
            q2 = jnp.concatenate([jnp.where(first_head, q, zero), jnp.where(first_head, zero, q)], axis=0)
            k = kd_ref[c, pl.ds(key_row0, kw), :]
            v = vd_ref[c, pl.ds(key_row0, kw), :]
            s = lax.dot_general(q2, k, (((1,), (1,)), ((), ())), preferred_element_type=_F32)
            s = s + bias_ref[br * 4 + case]
            m = jnp.max(s, axis=-1, keepdims=True)
            p = jnp.exp2(s - m)
            den = jnp.sum(p, axis=-1, keepdims=True)
            pv = jnp.dot(p.astype(_BF16), v, preferred_element_type=_F32)
            rows = pl.ds(q_start, bq, stride=d)
            o_ref[rows, :] = jnp.where(first_head, pv[0:bq], pv[bq:])
            m_ref[rows, :] = jnp.where(first_head, m[0:bq], m[bq:])
            den_ref[rows, :] = jnp.where(first_head, den[0:bq], den[bq:])
            return carry

        lax.fori_loop(0, d * n_sub, sub_block, 0, unroll=unroll)

    maxes = [r[...] for r in m_refs]
    top = functools.reduce(jnp.maximum, maxes)
    scales = [jnp.exp2(m - top) for m in maxes]
    total = functools.reduce(jnp.add, [w * r[...] for w, r in zip(scales, o_refs)])
    norm = functools.reduce(jnp.add, [w * r[...] for w, r in zip(scales, den_refs)])
    out_ref[0] = (total / norm).astype(out_ref.dtype)


def _dilated_attention(q, k, v, seq_len, tb, unroll):
    b, n_pairs, t, _ = q.shape
    n_heads = n_pairs * HEADS_PER_LANE_TILE
    bq = ATTN_QUERY_ROWS
    radius = DILATED_PATTERNS[0][0] // (2 * DILATED_PATTERNS[0][1])
    halo_rows = max((w // 2) for w, _ in DILATED_PATTERNS)
    assert tb % (bq * max(d for _, d in DILATED_PATTERNS)) == 0 and t % tb == 0 and tb % halo_rows == 0
    halo_per_block = tb // halo_rows
    n_halo = t // halo_rows
    cur = lambda bi, p, i: (bi, p, i, 0)
    prev = lambda bi, p, i: (bi, p, jnp.maximum(i * halo_per_block - 1, 0), 0)
    nxt = lambda bi, p, i: (bi, p, jnp.minimum((i + 1) * halo_per_block, n_halo - 1), 0)
    blk = lambda rows, imap: pl.BlockSpec((1, 1, rows, LANES), imap)
    gathered = [pltpu.VMEM((d, tb // d + 2 * radius, LANES), _BF16) for _, d in DILATED_PATTERNS]
    per_branch = [pltpu.VMEM((tb, LANES), _F32) for _ in DILATED_PATTERNS]
    quarter = pltpu.VMEM((GATHER_STRIDE, (tb + 2 * halo_rows) // GATHER_STRIDE, LANES), _F32)
    bias = pltpu.VMEM((4 * len(DILATED_PATTERNS), HEADS_PER_LANE_TILE * bq, bq + 2 * radius), _F32)
    return pl.pallas_call(
        functools.partial(_dilated_attention_body, seq_len=seq_len, n_heads=n_heads,
                          patterns=DILATED_PATTERNS, unroll=unroll),
        grid=(b, n_pairs, t // tb),
        in_specs=[blk(tb, cur), blk(halo_rows, prev), blk(tb, cur), blk(halo_rows, nxt),
                  blk(halo_rows, prev), blk(tb, cur), blk(halo_rows, nxt)],
        out_specs=pl.BlockSpec((1, tb, LANES), lambda bi, p, i: (bi, i, p)),
        out_shape=jax.ShapeDtypeStruct((b, t, n_pairs * LANES), _BF16),
        scratch_shapes=[gathered, gathered, per_branch, per_branch, per_branch, quarter, quarter, bias],
        compiler_params=pltpu.CompilerParams(
            dimension_semantics=("arbitrary", "arbitrary", "arbitrary"),
            vmem_limit_bytes=VMEM_LIMIT_BYTES),
        name="dilated_attention",
    )(q, k, k, k, v, v, v)


def _pool_outproj_body(attn_ref, u_ref, up_ref, un_ref, x_ref, pw_ref, ps_ref, wo_ref, out_ref, ext_ref,
                       *, seq_len):
    tm, attn_width = attn_ref.shape
    pool_width = u_ref.shape[1]
    group = pool_width // len(POOL_WINDOWS)
    acc = x_ref[...] + jnp.dot(attn_ref[...], wo_ref[0:attn_width, :], preferred_element_type=_F32)

    pos0 = (pl.program_id(0) * tm) % seq_len
    at_start = pos0 == 0
    at_end = pos0 + tm == seq_len
    ext_ref[0:POOL_HALO, :] = jnp.where(at_start, 0.0, up_ref[...])
    ext_ref[POOL_HALO:POOL_HALO + tm, :] = u_ref[...]
    ext_ref[POOL_HALO + tm:, :] = jnp.where(at_end, 0.0, un_ref[...])
    pos = pos0 + lax.broadcasted_iota(jnp.int32, (tm, 1), 0)
    for g, w in enumerate(POOL_WINDOWS):
        cols = slice(g * group, (g + 1) * group)
        half = w // 2
        total = ext_ref[POOL_HALO - half:POOL_HALO - half + tm, cols]
        for j in range(-half + 1, half):
            total = total + ext_ref[POOL_HALO + j:POOL_HALO + j + tm, cols]
        count = (jnp.minimum(pos + half, seq_len) - jnp.maximum(pos - half, 0)).astype(_F32)
        pooled = total / count - u_ref[:, cols]
        y = jnp.dot(pooled.astype(_BF16), pw_ref[g], preferred_element_type=_F32) * ps_ref[:, cols]
        acc = acc + jnp.dot(y.astype(_BF16), wo_ref[attn_width + g * group:attn_width + (g + 1) * group, :],
                            preferred_element_type=_F32)
    out_ref[...] = acc


def _pool_outproj(attn, u, x2d, pool_w_bf16, pool_scale, w_out_bf16, seq_len, tm):
    n, d_model = x2d.shape
    attn_width = attn.shape[1]
    pool_width = u.shape[1]
    halo_per_tile = tm // POOL_HALO
    n_halo = n // POOL_HALO
    row = lambda i: (i, 0)
    prev = lambda i: (jnp.maximum(i * halo_per_tile - 1, 0), 0)
    nxt = lambda i: (jnp.minimum((i + 1) * halo_per_tile, n_halo - 1), 0)
    const2 = lambda i: (0, 0)
    const3 = lambda i: (0, 0, 0)
    return pl.pallas_call(
        functools.partial(_pool_outproj_body, seq_len=seq_len),
        grid=(n // tm,),
        in_specs=[
            pl.BlockSpec((tm, attn_width), row),
            pl.BlockSpec((tm, pool_width), row),
            pl.BlockSpec((POOL_HALO, pool_width), prev),
            pl.BlockSpec((POOL_HALO, pool_width), nxt),
            pl.BlockSpec((tm, d_model), row),
            pl.BlockSpec(pool_w_bf16.shape, const3, pipeline_mode=pl.Buffered(1)),
            pl.BlockSpec((1, pool_width), const2),
            pl.BlockSpec(w_out_bf16.shape, const2, pipeline_mode=pl.Buffered(1)),
        ],
        out_specs=pl.BlockSpec((tm, d_model), row),
        out_shape=jax.ShapeDtypeStruct((n, d_model), _F32),
        scratch_shapes=[pltpu.VMEM((tm + 2 * POOL_HALO, pool_width), _F32)],
        compiler_params=pltpu.CompilerParams(
            dimension_semantics=("arbitrary",), vmem_limit_bytes=VMEM_LIMIT_BYTES),
        name="pool_outproj",
    )(attn, u, u, u, x2d, pool_w_bf16, pool_scale, w_out_bf16)


def _ffn_body(x_ref, g2_ref, w1_ref, w2_ref, gf_ref, out_ref, h_ref):
    j = pl.program_id(1)

    @pl.when(j == 0)
    def _():
        x = x_ref[...]
        h_ref[...] = (_rms_scale(x) * g2_ref[...]).astype(_BF16)
        out_ref[...] = x

    a = jnp.dot(h_ref[...], w1_ref[...], preferred_element_type=_F32)
    a = jnp.square(jnp.maximum(a, 0.0)).astype(_BF16)
    out_ref[...] += jnp.dot(a, w2_ref[...], preferred_element_type=_F32)

    @pl.when(j == pl.num_programs(1) - 1)
    def _():
        out_ref[...] = _rms_scale(out_ref[...]) * gf_ref[...]


def _ffn(x2d, g2, w1_bf16, w2_bf16, gf, tm, fc):
    n, d_model = x2d.shape
    d_ff = w1_bf16.shape[1]
    return pl.pallas_call(
        _ffn_body,
        grid=(n // tm, d_ff // fc),
        in_specs=[
            pl.BlockSpec((tm, d_model), lambda i, j: (i, 0)),
            pl.BlockSpec((1, d_model), lambda i, j: (0, 0)),
            pl.BlockSpec((d_model, fc), lambda i, j: (0, j)),
            pl.BlockSpec((fc, d_model), lambda i, j: (j, 0)),
            pl.BlockSpec((1, d_model), lambda i, j: (0, 0)),
        ],
        out_specs=pl.BlockSpec((tm, d_model), lambda i, j: (i, 0)),
        out_shape=jax.ShapeDtypeStruct((n, d_model), _F32),
        scratch_shapes=[pltpu.VMEM((tm, d_model), _BF16)],
        compiler_params=pltpu.CompilerParams(
            dimension_semantics=("arbitrary", "arbitrary"), vmem_limit_bytes=VMEM_LIMIT_BYTES),
        name="ffn",
    )(x2d, g2, w1_bf16, w2_bf16, gf)


def _tile_rows(seq_len, target):
    tm = min(seq_len, target)
    assert seq_len % tm == 0
    return tm


def _trunk(x, weights, final_g):
    b, t, d_model = x.shape
    x2d = x.reshape(b * t, d_model)
    for (ln1_g, w_in, pool_w, pool_scale, w_out, ln2_g, w1, w2) in weights:
        attn_width = (w_in.shape[1] - pool_w.shape[0] * pool_w.shape[1]) // 3
        tm = _tile_rows(t, 512)
        q, k, v, u = _inproj(x2d, ln1_g, w_in, attn_width, b, t, tm)
        attn = _dilated_attention(q, k, v, t, _tile_rows(t, 2048), unroll=16).reshape(b * t, attn_width)
        x2d = _pool_outproj(attn, u, x2d, pool_w, pool_scale, w_out, t, tm)
        x2d = _ffn(x2d, ln2_g, w1, w2, final_g, _tile_rows(t, 1024), fc=512)
    return x2d.reshape(b, t, d_model)


def kernel(x_prompt, x_sample, ln1_g, w_in, pool_w, pool_scale, w_out, ln2_g, w1, w2, final_g):
    depth = w_in.shape[0]
    assert depth == 1, "the fused final norm assumes a single layer"
    weights = [(
        ln1_g[l][None, :], w_in[l].astype(_BF16), pool_w[l].astype(_BF16), pool_scale[l][None, :],
        w_out[l].astype(_BF16), ln2_g[l][None, :], w1[l].astype(_BF16), w2[l].astype(_BF16),
    ) for l in range(depth)]
    gf = final_g[None, :]
    return (_trunk(x_prompt, weights, gf), _trunk(x_sample, weights, gf))
```

```python
import functools

import jax
import jax.numpy as jnp
from jax import lax
from jax.experimental import pallas as pl
from jax.experimental.pallas import tpu as pltpu

HEAD_DIM = 64
LANES = 128
HEADS_PER_LANE_TILE = LANES // HEAD_DIM
DILATED_PATTERNS = ((128, 1), (512, 4), (2048, 16))
ATTN_QUERY_ROWS = 128
GATHER_STRIDE = 4
POOL_WINDOWS = (2, 4, 8, 16)
POOL_HALO = 8
EPS = 1e-6
LOG2_E = 1.4426950408889634
VMEM_LIMIT_BYTES = 56 * 1024 * 1024

_F32 = jnp.float32
_BF16 = jnp.bfloat16


def _rms_scale(x):
    return x * lax.rsqrt(jnp.mean(x * x, axis=-1, keepdims=True) + EPS)


def _inproj_body(x_ref, g_ref, w_ref, *refs, attn_width):
    n_dil = len(DILATED_PATTERNS)
    qkv_refs = [refs[a * n_dil:(a + 1) * n_dil] for a in range(3)]
    u_ref, stage_ref, quarter_ref = refs[3 * n_dil:]
    tm = x_ref.shape[0]
    x = x_ref[...]
    h = (_rms_scale(x) * g_ref[...]).astype(_BF16)
    c = attn_width
    scales = (HEAD_DIM ** -0.5 * LOG2_E, None, None)
    for a, (d1_ref, d4_ref, d16_ref) in enumerate(qkv_refs):
        val = jnp.dot(h, w_ref[:, a * c:(a + 1) * c], preferred_element_type=_F32)
        if scales[a] is not None:
            val = val * scales[a]
        for p in range(c // LANES):
            slab = val[:, p * LANES:(p + 1) * LANES]
            stage_ref[p] = slab
            d1_ref[0, p, 0] = slab.astype(_BF16)
        for p in range(c // LANES):
            for c4 in range(GATHER_STRIDE):
                rows = stage_ref[p, pl.ds(c4, tm // GATHER_STRIDE, stride=GATHER_STRIDE), :]
                quarter_ref[p, c4] = rows
                d4_ref[0, p, c4] = rows.astype(_BF16)
            for c16 in range(GATHER_STRIDE ** 2):
                d16_ref[0, p, c16] = quarter_ref[
                    p, c16 % GATHER_STRIDE,
                    pl.ds(c16 // GATHER_STRIDE, tm // GATHER_STRIDE ** 2, stride=GATHER_STRIDE), :].astype(_BF16)
    u_ref[...] = jnp.dot(h, w_ref[:, 3 * c:], preferred_element_type=_F32)


def _inproj(x2d, g, w_in_bf16, attn_width, batch, seq_len, tm):
    n, d_model = x2d.shape
    in_width = w_in_bf16.shape[1]
    pool_width = in_width - 3 * attn_width
    n_pairs = attn_width // LANES
    tiles_per_seq = seq_len // tm
    dilations = [d for _, d in DILATED_PATTERNS]
    assert dilations == [1, GATHER_STRIDE, GATHER_STRIDE ** 2]
    row = lambda i: (i, 0)
    const = lambda i: (0, 0)
    qkv_specs = [pl.BlockSpec((1, n_pairs, d, tm // d, LANES),
                              lambda i: (i // tiles_per_seq, 0, 0, i % tiles_per_seq, 0)) for d in dilations]
    qkv_shapes = [jax.ShapeDtypeStruct((batch, n_pairs, d, seq_len // d, LANES), _BF16) for d in dilations]
    outs = pl.pallas_call(
        functools.partial(_inproj_body, attn_width=attn_width),
        grid=(n // tm,),
        in_specs=[
            pl.BlockSpec((tm, d_model), row),
            pl.BlockSpec((1, d_model), const),
            pl.BlockSpec((d_model, in_width), const, pipeline_mode=pl.Buffered(1)),
        ],
        out_specs=qkv_specs * 3 + [pl.BlockSpec((tm, pool_width), row)],
        out_shape=qkv_shapes * 3 + [jax.ShapeDtypeStruct((n, pool_width), _F32)],
        scratch_shapes=[pltpu.VMEM((n_pairs, tm, LANES), _F32),
                        pltpu.VMEM((n_pairs, GATHER_STRIDE, tm // GATHER_STRIDE, LANES), _F32)],
        compiler_params=pltpu.CompilerParams(
            dimension_semantics=("arbitrary",), vmem_limit_bytes=VMEM_LIMIT_BYTES),
        name="inproj",
    )(x2d, g, w_in_bf16)
    n_dil = len(dilations)
    return outs[0:n_dil], outs[n_dil:2 * n_dil], outs[2 * n_dil:3 * n_dil], outs[3 * n_dil]


def _dilated_attention_body(*refs, seq_len, n_heads, patterns):
    n_br = len(patterns)
    per_br = 7
    in_refs = [refs[br * per_br:(br + 1) * per_br] for br in range(n_br)]
    out_ref = refs[n_br * per_br]
    o_refs, m_refs, den_refs, bias_ref = refs[n_br * per_br + 1:]
    tb = out_ref.shape[1]
    bq = ATTN_QUERY_ROWS
    n_stack = HEADS_PER_LANE_TILE
    pair = pl.program_id(0)
    block = pl.program_id(2)
    radius = patterns[0][0] // (2 * patterns[0][1])
    assert all(w // (2 * d) == radius for w, d in patterns)
    kw = bq + 2 * radius
    dilations = [d for _, d in patterns]

    @pl.when((pl.program_id(1) == 0) & (block == 0))
    def _():
        row = lax.broadcasted_iota(jnp.int32, (bq, kw), 0)
        col = lax.broadcasted_iota(jnp.int32, (bq, kw), 1)
        dist = jnp.abs(col - radius - row)
        first_head_index = (pair * n_stack + 1).astype(_F32)
        slopes = [jnp.exp2((jnp.full((1, LANES), sub, _F32) + first_head_index) * (-8.0 / n_heads))[:, 0:1]
                  for sub in range(n_stack)]
        for case in range(4):
            ok = dist <= radius
            if case & 1:
                ok = ok & (col >= radius)
            if case & 2:
                ok = ok & (col < radius + bq)
            masked_dist = jnp.where(ok, -dist.astype(_F32), -jnp.inf)
            for br, d in enumerate(dilations):
                for sub in range(n_stack):
                    bias_ref[br * 4 + case, sub * bq:(sub + 1) * bq, :] = (
                        masked_dist * (slopes[sub] * (d * LOG2_E)))

    lane = lax.broadcasted_iota(jnp.int32, (bq, LANES), 1)
    first_head = lane < HEAD_DIM
    ones = jnp.ones((kw, LANES), _BF16)

    def window(prev_ref, cur_ref, next_ref, c, j, n_sub):
        lo, hi = j * bq - radius, (j + 1) * bq + radius
        parts = []
        if j == 0:
            parts.append(prev_ref[0, 0, c])
            lo = 0
        parts.append(cur_ref[0, 0, c, lo:min(hi, n_sub * bq), :])
        if j == n_sub - 1:
            parts.append(next_ref[0, 0, c])
        return parts[0] if len(parts) == 1 else jnp.concatenate(parts, axis=0)

    for br, d in enumerate(dilations):
        n_sub = tb // (d * bq)
        subs_per_class = seq_len // (d * bq)
        q_ref, kp_ref, kc_ref, kn_ref, vp_ref, vc_ref, vn_ref = in_refs[br]
        o_ref, m_ref, den_ref = o_refs[br], m_refs[br], den_refs[br]
        for c in range(d):
            for j in range(n_sub):
                case = 0
                if j == 0:
                    case = case + (block == 0).astype(jnp.int32)
                if j == n_sub - 1:
                    case = case + 2 * (block * n_sub + j == subs_per_class - 1).astype(jnp.int32)
                q = q_ref[0, 0, c, j * bq:(j + 1) * bq, :]
                zero = jnp.zeros_like(q)
                q2 = jnp.concatenate([jnp.where(first_head, q, zero), jnp.where(first_head, zero, q)], axis=0)
                k = window(kp_ref, kc_ref, kn_ref, c, j, n_sub)
                v = window(vp_ref, vc_ref, vn_ref, c, j, n_sub)
                s = lax.dot_general(q2, k, (((1,), (1,)), ((), ())), preferred_element_type=_F32)
                s = s + bias_ref[br * 4 + case]
                m = jnp.max(s, axis=-1, keepdims=True)
                p = jnp.exp2(s - m).astype(_BF16)
                pv = jnp.dot(p, jnp.concatenate([v, ones], axis=1), preferred_element_type=_F32)
                rows = pl.ds(c + d * bq * j, bq, stride=d)
                o_ref[rows, :] = jnp.where(first_head, pv[0:bq, 0:LANES], pv[bq:, 0:LANES])
                den_ref[rows, :] = jnp.where(first_head, pv[0:bq, LANES:], pv[bq:, LANES:])
                m_ref[rows, :] = jnp.where(first_head, m[0:bq], m[bq:])

    maxes = [r[...] for r in m_refs]
    top = functools.reduce(jnp.maximum, maxes)
    scales = [jnp.exp2(m - top) for m in maxes]
    total = functools.reduce(jnp.add, [w * r[...] for w, r in zip(scales, o_refs)])
    norm = functools.reduce(jnp.add, [w * r[...] for w, r in zip(scales, den_refs)])
    out_ref[0] = (total / norm).astype(out_ref.dtype)


def _dilated_attention(qs, ks, vs, seq_len, tb):
    b, n_pairs = qs[0].shape[0:2]
    n_heads = n_pairs * HEADS_PER_LANE_TILE
    bq = ATTN_QUERY_ROWS
    radius = DILATED_PATTERNS[0][0] // (2 * DILATED_PATTERNS[0][1])
    assert tb % (bq * max(d for _, d in DILATED_PATTERNS)) == 0 and seq_len % tb == 0
    in_specs, operands = [], []
    for (_, d), q, k, v in zip(DILATED_PATTERNS, qs, ks, vs):
        rows = tb // d
        halo_per_block = rows // radius
        n_halo = seq_len // d // radius
        cur = lambda p, bi, i: (bi, p, 0, i, 0)
        prev = lambda p, bi, i, h=halo_per_block: (bi, p, 0, jnp.maximum(i * h - 1, 0), 0)
        nxt = lambda p, bi, i, h=halo_per_block, n=n_halo: (bi, p, 0, jnp.minimum((i + 1) * h, n - 1), 0)
        blk = lambda r, imap, d=d: pl.BlockSpec((1, 1, d, r, LANES), imap)
        in_specs += [blk(rows, cur), blk(radius, prev), blk(rows, cur), blk(radius, nxt),
                     blk(radius, prev), blk(rows, cur), blk(radius, nxt)]
        operands += [q, k, k, k, v, v, v]
    per_branch = [pltpu.VMEM((tb, LANES), _F32) for _ in DILATED_PATTERNS]
    bias = pltpu.VMEM((4 * len(DILATED_PATTERNS), HEADS_PER_LANE_TILE * bq, bq + 2 * radius), _F32)
    return pl.pallas_call(
        functools.partial(_dilated_attention_body, seq_len=seq_len, n_heads=n_heads,
                          patterns=DILATED_PATTERNS),
        grid=(n_pairs, b, seq_len // tb),
        in_specs=in_specs,
        out_specs=pl.BlockSpec((1, tb, LANES), lambda p, bi, i: (bi, i, p)),
        out_shape=jax.ShapeDtypeStruct((b, seq_len, n_pairs * LANES), _BF16),
        scratch_shapes=[per_branch, per_branch, per_branch, bias],
        compiler_params=pltpu.CompilerParams(
            dimension_semantics=("arbitrary", "arbitrary", "arbitrary"),
            vmem_limit_bytes=VMEM_LIMIT_BYTES),
        name="dilated_attention",
    )(*operands)


def _pool_outproj_body(attn_ref, u_ref, up_ref, un_ref, x_ref, pw_ref, ps_ref, wo_ref, out_ref, ext_ref,
                       *, seq_len):
    tm, attn_width = attn_ref.shape
    pool_width = u_ref.shape[1]
    group = pool_width // len(POOL_WINDOWS)

    pos0 = (pl.program_id(0) * tm) % seq_len
    at_start = pos0 == 0
    at_end = pos0 + tm == seq_len
    ext_ref[0:POOL_HALO, :] = jnp.where(at_start, 0.0, up_ref[...])
    ext_ref[POOL_HALO:POOL_HALO + tm, :] = u_ref[...]
    ext_ref[POOL_HALO + tm:, :] = jnp.where(at_end, 0.0, un_ref[...])
    pos = pos0 + lax.broadcasted_iota(jnp.int32, (tm, 1), 0)
    mixed = [attn_ref[...]]
    for g, w in enumerate(POOL_WINDOWS):
        cols = slice(g * group, (g + 1) * group)
        half = w // 2
        total = ext_ref[POOL_HALO - half:POOL_HALO - half + tm, cols]
        for j in range(-half + 1, half):
            total = total + ext_ref[POOL_HALO + j:POOL_HALO + j + tm, cols]
        count = (jnp.minimum(pos + half, seq_len) - jnp.maximum(pos - half, 0)).astype(_F32)
        pooled = total / count - u_ref[:, cols]
        y = jnp.dot(pooled.astype(_BF16), pw_ref[g], preferred_element_type=_F32) * ps_ref[:, cols]
        mixed.append(y.astype(_BF16))
    out_ref[...] = x_ref[...] + jnp.dot(jnp.concatenate(mixed, axis=1), wo_ref[...],
                                        preferred_element_type=_F32)


def _pool_outproj(attn, u, x2d, pool_w_bf16, pool_scale, w_out_bf16, seq_len, tm):
    n, d_model = x2d.shape
    attn_width = attn.shape[1]
    pool_width = u.shape[1]
    halo_per_tile = tm // POOL_HALO
    n_halo = n // POOL_HALO
    row = lambda i: (i, 0)
    prev = lambda i: (jnp.maximum(i * halo_per_tile - 1, 0), 0)
    nxt = lambda i: (jnp.minimum((i + 1) * halo_per_tile, n_halo - 1), 0)
    const2 = lambda i: (0, 0)
    const3 = lambda i: (0, 0, 0)
    return pl.pallas_call(
        functools.partial(_pool_outproj_body, seq_len=seq_len),
        grid=(n // tm,),
        in_specs=[
            pl.BlockSpec((tm, attn_width), row),
            pl.BlockSpec((tm, pool_width), row),
            pl.BlockSpec((POOL_HALO, pool_width), prev),
            pl.BlockSpec((POOL_HALO, pool_width), nxt),
            pl.BlockSpec((tm, d_model), row),
            pl.BlockSpec(pool_w_bf16.shape, const3, pipeline_mode=pl.Buffered(1)),
            pl.BlockSpec((1, pool_width), const2),
            pl.BlockSpec(w_out_bf16.shape, const2, pipeline_mode=pl.Buffered(1)),
        ],
        out_specs=pl.BlockSpec((tm, d_model), row),
        out_shape=jax.ShapeDtypeStruct((n, d_model), _F32),
        scratch_shapes=[pltpu.VMEM((tm + 2 * POOL_HALO, pool_width), _F32)],
        compiler_params=pltpu.CompilerParams(
            dimension_semantics=("arbitrary",), vmem_limit_bytes=VMEM_LIMIT_BYTES),
        name="pool_outproj",
    )(attn, u, u, u, x2d, pool_w_bf16, pool_scale, w_out_bf16)


def _ffn_body(x_ref, g2_ref, w1_ref, w2_ref, gf_ref, out_ref, h_ref):
    j = pl.program_id(1)

    @pl.when(j == 0)
    def _():
        x = x_ref[...]
        h_ref[...] = (_rms_scale(x) * g2_ref[...]).astype(_BF16)
        out_ref[...] = x

    a = jnp.dot(h_ref[...], w1_ref[...], preferred_element_type=_F32)
    a = jnp.square(jnp.maximum(a, 0.0)).astype(_BF16)
    out_ref[...] += jnp.dot(a, w2_ref[...], preferred_element_type=_F32)

    @pl.when(j == pl.num_programs(1) - 1)
    def _():
        out_ref[...] = _rms_scale(out_ref[...]) * gf_ref[...]


def _ffn(x2d, g2, w1_bf16, w2_bf16, gf, tm, fc):
    n, d_model = x2d.shape
    d_ff = w1_bf16.shape[1]
    return pl.pallas_call(
        _ffn_body,
        grid=(n // tm, d_ff // fc),
        in_specs=[
            pl.BlockSpec((tm, d_model), lambda i, j: (i, 0)),
            pl.BlockSpec((1, d_model), lambda i, j: (0, 0)),
            pl.BlockSpec((d_model, fc), lambda i, j: (0, j)),
            pl.BlockSpec((fc, d_model), lambda i, j: (j, 0)),
            pl.BlockSpec((1, d_model), lambda i, j: (0, 0)),
        ],
        out_specs=pl.BlockSpec((tm, d_model), lambda i, j: (i, 0)),
        out_shape=jax.ShapeDtypeStruct((n, d_model), _F32),
        scratch_shapes=[pltpu.VMEM((tm, d_model), _BF16)],
        compiler_params=pltpu.CompilerParams(
            dimension_semantics=("arbitrary", "arbitrary"), vmem_limit_bytes=VMEM_LIMIT_BYTES),
        name="ffn",
    )(x2d, g2, w1_bf16, w2_bf16, gf)


def _tile_rows(seq_len, target):
    tm = min(seq_len, target)
    assert seq_len % tm == 0
    return tm


def _trunk(x, weights, final_g):
    b, t, d_model = x.shape
    x2d = x.reshape(b * t, d_model)
    for (ln1_g, w_in, pool_w, pool_scale, w_out, ln2_g, w1, w2) in weights:
        attn_width = (w_in.shape[1] - pool_w.shape[0] * pool_w.shape[1]) // 3
        tm = _tile_rows(t, 512)
        qs, ks, vs, u = _inproj(x2d, ln1_g, w_in, attn_width, b, t, tm)
        attn = _dilated_attention(qs, ks, vs, t, _tile_rows(t, 2048)).reshape(b * t, attn_width)
        x2d = _pool_outproj(attn, u, x2d, pool_w, pool_scale, w_out, t, tm)
        x2d = _ffn(x2d, ln2_g, w1, w2, final_g, _tile_rows(t, 1024), fc=512)
    return x2d.reshape(b, t, d_model)


def kernel(x_prompt, x_sample, ln1_g, w_in, pool_w, pool_scale, w_out, ln2_g, w1, w2, final_g):
    depth = w_in.shape[0]
    assert depth == 1, "the fused final norm assumes a single layer"
    weights = [(
        ln1_g[l][None, :], w_in[l].astype(_BF16), pool_w[l].astype(_BF16), pool_scale[l][None, :],
        w_out[l].astype(_BF16), ln2_g[l][None, :], w1[l].astype(_BF16), w2[l].astype(_BF16),
    ) for l in range(depth)]
    gf = final_g[None, :]
    return (_trunk(x_prompt, weights, gf), _trunk(x_sample, weights, gf))
```

```python
import functools

import jax
import jax.numpy as jnp
from jax import lax
from jax.experimental import pallas as pl
from jax.experimental.pallas import tpu as pltpu

HEAD_DIM = 64
LANES = 128
HEADS_PER_LANE_TILE = LANES // HEAD_DIM
DILATED_PATTERNS = ((128, 1), (512, 4), (2048, 16))
ATTN_QUERY_ROWS = 128
GATHER_STRIDE = 4
POOL_WINDOWS = (2, 4, 8, 16)
POOL_HALO = 8
EPS = 1e-6
LOG2_E = 1.4426950408889634
VMEM_LIMIT_BYTES = 56 * 1024 * 1024

_F32 = jnp.float32
_BF16 = jnp.bfloat16


def _rms_scale(x):
    return x * lax.rsqrt(jnp.mean(x * x, axis=-1, keepdims=True) + EPS)


def _inproj_body(x_ref, g_ref, w_ref, *refs, attn_width):
    n_dil = len(DILATED_PATTERNS)
    qkv_refs = [refs[a * n_dil:(a + 1) * n_dil] for a in range(3)]
    u_ref, stage_ref, quarter_ref = refs[3 * n_dil:]
    tm = x_ref.shape[0]
    x = x_ref[...]
    h = (_rms_scale(x) * g_ref[...]).astype(_BF16)
    c = attn_width
    scales = (HEAD_DIM ** -0.5 * LOG2_E, None, None)
    for a, (d1_ref, d4_ref, d16_ref) in enumerate(qkv_refs):
        val = jnp.dot(h, w_ref[:, a * c:(a + 1) * c], preferred_element_type=_F32)
        if scales[a] is not None:
            val = val * scales[a]
        for p in range(c // LANES):
            slab = val[:, p * LANES:(p + 1) * LANES]
            stage_ref[p] = slab
            d1_ref[0, p, 0] = slab.astype(_BF16)
        for p in range(c // LANES):
            for c4 in range(GATHER_STRIDE):
                rows = stage_ref[p, pl.ds(c4, tm // GATHER_STRIDE, stride=GATHER_STRIDE), :]
                quarter_ref[p, c4] = rows
                d4_ref[0, p, c4] = rows.astype(_BF16)
            for c16 in range(GATHER_STRIDE ** 2):
                d16_ref[0, p, c16] = quarter_ref[
                    p, c16 % GATHER_STRIDE,
                    pl.ds(c16 // GATHER_STRIDE, tm // GATHER_STRIDE ** 2, stride=GATHER_STRIDE), :].astype(_BF16)
    u_ref[...] = jnp.dot(h, w_ref[:, 3 * c:], preferred_element_type=_F32)


def _inproj(x2d, g, w_in_bf16, attn_width, batch, seq_len, tm):
    n, d_model = x2d.shape
    in_width = w_in_bf16.shape[1]
    pool_width = in_width - 3 * attn_width
    n_pairs = attn_width // LANES
    tiles_per_seq = seq_len // tm
    dilations = [d for _, d in DILATED_PATTERNS]
    assert dilations == [1, GATHER_STRIDE, GATHER_STRIDE ** 2]
    row = lambda i: (i, 0)
    const = lambda i: (0, 0)
    qkv_specs = [pl.BlockSpec((1, n_pairs, d, tm // d, LANES),
                              lambda i: (i // tiles_per_seq, 0, 0, i % tiles_per_seq, 0)) for d in dilations]
    qkv_shapes = [jax.ShapeDtypeStruct((batch, n_pairs, d, seq_len // d, LANES), _BF16) for d in dilations]
    outs = pl.pallas_call(
        functools.partial(_inproj_body, attn_width=attn_width),
        grid=(n // tm,),
        in_specs=[
            pl.BlockSpec((tm, d_model), row),
            pl.BlockSpec((1, d_model), const),
            pl.BlockSpec((d_model, in_width), const, pipeline_mode=pl.Buffered(1)),
        ],
        out_specs=qkv_specs * 3 + [pl.BlockSpec((tm, pool_width), row)],
        out_shape=qkv_shapes * 3 + [jax.ShapeDtypeStruct((n, pool_width), _F32)],
        scratch_shapes=[pltpu.VMEM((n_pairs, tm, LANES), _F32),
                        pltpu.VMEM((n_pairs, GATHER_STRIDE, tm // GATHER_STRIDE, LANES), _F32)],
        compiler_params=pltpu.CompilerParams(
            dimension_semantics=("arbitrary",), vmem_limit_bytes=VMEM_LIMIT_BYTES),
        name="inproj",
    )(x2d, g, w_in_bf16)
    n_dil = len(dilations)
    return outs[0:n_dil], outs[n_dil:2 * n_dil], outs[2 * n_dil:3 * n_dil], outs[3 * n_dil]


def _dilated_attention_body(*refs, seq_len, n_heads, patterns):
    n_br = len(patterns)
    per_br = 7
    in_refs = [refs[br * per_br:(br + 1) * per_br] for br in range(n_br)]
    out_ref = refs[n_br * per_br]
    o_refs, m_refs, den_refs, bias_ref = refs[n_br * per_br + 1:]
    tb = out_ref.shape[1]
    bq = ATTN_QUERY_ROWS
    n_stack = HEADS_PER_LANE_TILE
    pair = pl.program_id(0)
    block = pl.program_id(2)
    radius = patterns[0][0] // (2 * patterns[0][1])
    assert all(w // (2 * d) == radius for w, d in patterns)
    kw = bq + 2 * radius
    dilations = [d for _, d in patterns]

    @pl.when((pl.program_id(1) == 0) & (block == 0))
    def _():
        row = lax.broadcasted_iota(jnp.int32, (bq, kw), 0)
        col = lax.broadcasted_iota(jnp.int32, (bq, kw), 1)
        dist = jnp.abs(col - radius - row)
        first_head_index = (pair * n_stack + 1).astype(_F32)
        slopes = [jnp.exp2((jnp.full((1, LANES), sub, _F32) + first_head_index) * (-8.0 / n_heads))[:, 0:1]
                  for sub in range(n_stack)]
        for case in range(4):
            ok = dist <= radius
            if case & 1:
                ok = ok & (col >= radius)
            if case & 2:
                ok = ok & (col < radius + bq)
            masked_dist = jnp.where(ok, -dist.astype(_F32), -jnp.inf)
            for br, d in enumerate(dilations):
                for sub in range(n_stack):
                    bias_ref[br * 4 + case, sub * bq:(sub + 1) * bq, :] = (
                        masked_dist * (slopes[sub] * (d * LOG2_E)))

    lane = lax.broadcasted_iota(jnp.int32, (bq, LANES), 1)
    first_head = lane < HEAD_DIM
    ones = jnp.ones((kw, LANES), _BF16)

    def window(prev_ref, cur_ref, next_ref, c, j, n_sub):
        lo, hi = j * bq - radius, (j + 1) * bq + radius
        parts = []
        if j == 0:
            parts.append(prev_ref[0, 0, c])
            lo = 0
        parts.append(cur_ref[0, 0, c, lo:min(hi, n_sub * bq), :])
        if j == n_sub - 1:
            parts.append(next_ref[0, 0, c])
        return parts[0] if len(parts) == 1 else jnp.concatenate(parts, axis=0)

    for br, d in enumerate(dilations):
        n_sub = tb // (d * bq)
        subs_per_class = seq_len // (d * bq)
        q_ref, kp_ref, kc_ref, kn_ref, vp_ref, vc_ref, vn_ref = in_refs[br]
        o_ref, m_ref, den_ref = o_refs[br], m_refs[br], den_refs[br]
        for c in range(d):
            for j in range(n_sub):
                case = 0
                if j == 0:
                    case = case + (block == 0).astype(jnp.int32)
                if j == n_sub - 1:
                    case = case + 2 * (block * n_sub + j == subs_per_class - 1).astype(jnp.int32)
                q = q_ref[0, 0, c, j * bq:(j + 1) * bq, :]
                zero = jnp.zeros_like(q)
                q2 = jnp.concatenate([jnp.where(first_head, q, zero), jnp.where(first_head, zero, q)], axis=0)
                k = window(kp_ref, kc_ref, kn_ref, c, j, n_sub)
                v = window(vp_ref, vc_ref, vn_ref, c, j, n_sub)
                s = lax.dot_general(q2, k, (((1,), (1,)), ((), ())), preferred_element_type=_F32)
                s = s + bias_ref[br * 4 + case]
                m = jnp.max(s, axis=-1, keepdims=True)
                p = jnp.exp2(s - m).astype(_BF16)
                pv = jnp.dot(p, jnp.concatenate([v, ones], axis=1), preferred_element_type=_F32)
                rows = pl.ds(c + d * bq * j, bq, stride=d)
                o_ref[rows, :] = jnp.where(first_head, pv[0:bq, 0:LANES], pv[bq:, 0:LANES])
                den_ref[rows, :] = jnp.where(first_head, pv[0:bq, LANES:], pv[bq:, LANES:])
                m_ref[rows, :] = jnp.where(first_head, m[0:bq], m[bq:])

    maxes = [r[...] for r in m_refs]
    top = functools.reduce(jnp.maximum, maxes)
    scales = [jnp.exp2(m - top) for m in maxes]
    total = functools.reduce(jnp.add, [w * r[...] for w, r in zip(scales, o_refs)])
    norm = functools.reduce(jnp.add, [w * r[...] for w, r in zip(scales, den_refs)])
    out_ref[0] = (total / norm).astype(out_ref.dtype)


def _dilated_attention(qs, ks, vs, seq_len, tb):
    b, n_pairs = qs[0].shape[0:2]
    n_heads = n_pairs * HEADS_PER_LANE_TILE
    bq = ATTN_QUERY_ROWS
    radius = DILATED_PATTERNS[0][0] // (2 * DILATED_PATTERNS[0][1])
    assert tb % (bq * max(d for _, d in DILATED_PATTERNS)) == 0 and seq_len % tb == 0
    in_specs, operands = [], []
    for (_, d), q, k, v in zip(DILATED_PATTERNS, qs, ks, vs):
        rows = tb // d
        halo_per_block = rows // radius
        n_halo = seq_len // d // radius
        cur = lambda p, bi, i: (bi, p, 0, i, 0)
        prev = lambda p, bi, i, h=halo_per_block: (bi, p, 0, jnp.maximum(i * h - 1, 0), 0)
        nxt = lambda p, bi, i, h=halo_per_block, n=n_halo: (bi, p, 0, jnp.minimum((i + 1) * h, n - 1), 0)
        blk = lambda r, imap, d=d: pl.BlockSpec((1, 1, d, r, LANES), imap)
        in_specs += [blk(rows, cur), blk(radius, prev), blk(rows, cur), blk(radius, nxt),
                     blk(radius, prev), blk(rows, cur), blk(radius, nxt)]
        operands += [q, k, k, k, v, v, v]
    per_branch = [pltpu.VMEM((tb, LANES), _F32) for _ in DILATED_PATTERNS]
    bias = pltpu.VMEM((4 * len(DILATED_PATTERNS), HEADS_PER_LANE_TILE * bq, bq + 2 * radius), _F32)
    return pl.pallas_call(
        functools.partial(_dilated_attention_body, seq_len=seq_len, n_heads=n_heads,
                          patterns=DILATED_PATTERNS),
        grid=(n_pairs, b, seq_len // tb),
        in_specs=in_specs,
        out_specs=pl.BlockSpec((1, tb, LANES), lambda p, bi, i: (bi, i, p)),
        out_shape=jax.ShapeDtypeStruct((b, seq_len, n_pairs * LANES), _BF16),
        scratch_shapes=[per_branch, per_branch, per_branch, bias],
        compiler_params=pltpu.CompilerParams(
            dimension_semantics=("arbitrary", "arbitrary", "arbitrary"),
            vmem_limit_bytes=VMEM_LIMIT_BYTES),
        name="dilated_attention",
    )(*operands)


def _window_sum(ext_ref, run2_ref, run4_ref, cols, w, tm):
    lo = POOL_HALO - w // 2
    if w <= 4:
        total = ext_ref[lo:lo + tm, cols]
        for j in range(1, w):
            total = total + ext_ref[lo + j:lo + j + tm, cols]
        return total
    n = tm + 2 * POOL_HALO
    run2_ref[0:n, :] = ext_ref[0:n, cols] + ext_ref[1:n + 1, cols]
    run4_ref[0:n, :] = run2_ref[0:n, :] + run2_ref[2:n + 2, :]
    if w == 8:
        return run4_ref[lo:lo + tm, :] + run4_ref[lo + 4:lo + 4 + tm, :]
    assert w == 16 and lo == 0
    run8 = run4_ref[0:tm + 8, :] + run4_ref[4:tm + 12, :]
    return run8[0:tm] + run8[8:tm + 8]


def _pool_outproj_body(attn_ref, u_ref, up_ref, un_ref, x_ref, pw_ref, ps_ref, wo_ref, out_ref,
                       ext_ref, run2_ref, run4_ref, *, seq_len):
    tm, attn_width = attn_ref.shape
    pool_width = u_ref.shape[1]
    group = pool_width // len(POOL_WINDOWS)

    pos0 = (pl.program_id(0) * tm) % seq_len
    at_start = pos0 == 0
    at_end = pos0 + tm == seq_len
    ext_ref[0:POOL_HALO, :] = jnp.where(at_start, 0.0, up_ref[...])
    ext_ref[POOL_HALO:POOL_HALO + tm, :] = u_ref[...]
    ext_ref[POOL_HALO + tm:2 * POOL_HALO + tm, :] = jnp.where(at_end, 0.0, un_ref[...])
    ext_ref[2 * POOL_HALO + tm:, :] = jnp.zeros((POOL_HALO, pool_width), _F32)
    run2_ref[2 * POOL_HALO + tm:, :] = jnp.zeros((POOL_HALO, group), _F32)
    pos = pos0 + lax.broadcasted_iota(jnp.int32, (tm, 1), 0)
    out_ref[...] = x_ref[...] + jnp.dot(attn_ref[...], wo_ref[0:attn_width, :], preferred_element_type=_F32)
    mixed = []
    for g, w in enumerate(POOL_WINDOWS):
        cols = slice(g * group, (g + 1) * group)
        half = w // 2
        total = _window_sum(ext_ref, run2_ref, run4_ref, cols, w, tm)
        count = (jnp.minimum(pos + half, seq_len) - jnp.maximum(pos - half, 0)).astype(_F32)
        pooled = total / count - u_ref[:, cols]
        y = jnp.dot(pooled.astype(_BF16), pw_ref[g], preferred_element_type=_F32) * ps_ref[:, cols]
        mixed.append(y.astype(_BF16))
    out_ref[...] += jnp.dot(jnp.concatenate(mixed, axis=1), wo_ref[attn_width:, :],
                            preferred_element_type=_F32)


def _pool_outproj(attn, u, x2d, pool_w_bf16, pool_scale, w_out_bf16, seq_len, tm):
    n, d_model = x2d.shape
    attn_width = attn.shape[1]
    pool_width = u.shape[1]
    halo_per_tile = tm // POOL_HALO
    n_halo = n // POOL_HALO
    row = lambda i: (i, 0)
    prev = lambda i: (jnp.maximum(i * halo_per_tile - 1, 0), 0)
    nxt = lambda i: (jnp.minimum((i + 1) * halo_per_tile, n_halo - 1), 0)
    const2 = lambda i: (0, 0)
    const3 = lambda i: (0, 0, 0)
    return pl.pallas_call(
        functools.partial(_pool_outproj_body, seq_len=seq_len),
        grid=(n // tm,),
        in_specs=[
            pl.BlockSpec((tm, attn_width), row),
            pl.BlockSpec((tm, pool_width), row),
            pl.BlockSpec((POOL_HALO, pool_width), prev),
            pl.BlockSpec((POOL_HALO, pool_width), nxt),
            pl.BlockSpec((tm, d_model), row),
            pl.BlockSpec(pool_w_bf16.shape, const3, pipeline_mode=pl.Buffered(1)),
            pl.BlockSpec((1, pool_width), const2),
            pl.BlockSpec(w_out_bf16.shape, const2, pipeline_mode=pl.Buffered(1)),
        ],
        out_specs=pl.BlockSpec((tm, d_model), row),
        out_shape=jax.ShapeDtypeStruct((n, d_model), _F32),
        scratch_shapes=[pltpu.VMEM((tm + 3 * POOL_HALO, pool_width), _F32),
                        pltpu.VMEM((tm + 3 * POOL_HALO, pool_width // len(POOL_WINDOWS)), _F32),
                        pltpu.VMEM((tm + 2 * POOL_HALO, pool_width // len(POOL_WINDOWS)), _F32)],
        compiler_params=pltpu.CompilerParams(
            dimension_semantics=("arbitrary",), vmem_limit_bytes=VMEM_LIMIT_BYTES),
        name="pool_outproj",
    )(attn, u, u, u, x2d, pool_w_bf16, pool_scale, w_out_bf16)


def _ffn_body(x_ref, g2_ref, w1_ref, w2_ref, gf_ref, out_ref, h_ref):
    j = pl.program_id(1)

    @pl.when(j == 0)
    def _():
        x = x_ref[...]
        h_ref[...] = (_rms_scale(x) * g2_ref[...]).astype(_BF16)
        out_ref[...] = x

    a = jnp.dot(h_ref[...], w1_ref[...], preferred_element_type=_F32)
    a = jnp.square(jnp.maximum(a, 0.0)).astype(_BF16)
    out_ref[...] += jnp.dot(a, w2_ref[...], preferred_element_type=_F32)

    @pl.when(j == pl.num_programs(1) - 1)
    def _():
        out_ref[...] = _rms_scale(out_ref[...]) * gf_ref[...]


def _ffn(x2d, g2, w1_bf16, w2_bf16, gf, tm, fc):
    n, d_model = x2d.shape
    d_ff = w1_bf16.shape[1]
    return pl.pallas_call(
        _ffn_body,
        grid=(n // tm, d_ff // fc),
        in_specs=[
            pl.BlockSpec((tm, d_model), lambda i, j: (i, 0)),
            pl.BlockSpec((1, d_model), lambda i, j: (0, 0)),
            pl.BlockSpec((d_model, fc), lambda i, j: (0, j)),
            pl.BlockSpec((fc, d_model), lambda i, j: (j, 0)),
            pl.BlockSpec((1, d_model), lambda i, j: (0, 0)),
        ],
        out_specs=pl.BlockSpec((tm, d_model), lambda i, j: (i, 0)),
        out_shape=jax.ShapeDtypeStruct((n, d_model), _F32),
        scratch_shapes=[pltpu.VMEM((tm, d_model), _BF16)],
        compiler_params=pltpu.CompilerParams(
            dimension_semantics=("arbitrary", "arbitrary"), vmem_limit_bytes=62 * 1024 * 1024),
        name="ffn",
    )(x2d, g2, w1_bf16, w2_bf16, gf)


def _tile_rows(seq_len, target):
    tm = min(seq_len, target)
    assert seq_len % tm == 0
    return tm


def _trunk(x, weights, final_g):
    b, t, d_model = x.shape
    x2d = x.reshape(b * t, d_model)
    for (ln1_g, w_in, pool_w, pool_scale, w_out, ln2_g, w1, w2) in weights:
        attn_width = (w_in.shape[1] - pool_w.shape[0] * pool_w.shape[1]) // 3
        tm = _tile_rows(t, 512)
        qs, ks, vs, u = _inproj(x2d, ln1_g, w_in, attn_width, b, t, tm)
        attn = _dilated_attention(qs, ks, vs, t, _tile_rows(t, 2048)).reshape(b * t, attn_width)
        x2d = _pool_outproj(attn, u, x2d, pool_w, pool_scale, w_out, t, tm)
        x2d = _ffn(x2d, ln2_g, w1, w2, final_g, _tile_rows(t, 1024), fc=1024)
    return x2d.reshape(b, t, d_model)


def kernel(x_prompt, x_sample, ln1_g, w_in, pool_w, pool_scale, w_out, ln2_g, w1, w2, final_g):
    depth = w_in.shape[0]
    assert depth == 1, "the fused final norm assumes a single layer"
    weights = [(
        ln1_g[l][None, :], w_in[l].astype(_BF16), pool_w[l].astype(_BF16), pool_scale[l][None, :],
        w_out[l].astype(_BF16), ln2_g[l][None, :], w1[l].astype(_BF16), w2[l].astype(_BF16),
    ) for l in range(depth)]
    gf = final_g[None, :]
    return (_trunk(x_prompt, weights, gf), _trunk(x_sample, weights, gf))
```
